```python
import jax, jax.numpy as jnp
from jax import lax
import numpy as np

D_MODEL = 1024
BATCH = 1
SEQ = 16384
DEPTH = 2

LRU_WIDTH = D_MODEL
LRU_HEADS = 8
LRU_HEAD_DIM = LRU_WIDTH // LRU_HEADS
LRU_CONV = 4
LRU_C = 8.0
SC_WIDTH = D_MODEL
SC_HEADS = 8
SC_CONV = 3
POOL_WIDTH = 2 * D_MODEL
POOL_WINDOWS = (2, 4, 8, 16)
POOL_GROUPS = len(POOL_WINDOWS)
POOL_GROUP_DIM = POOL_WIDTH // POOL_GROUPS
EVEN_IN = 2 * LRU_WIDTH + 4 * SC_WIDTH
EVEN_SPLITS = (LRU_WIDTH, 2 * LRU_WIDTH, 2 * LRU_WIDTH + SC_WIDTH,
               2 * LRU_WIDTH + 2 * SC_WIDTH, 2 * LRU_WIDTH + 3 * SC_WIDTH)
N_EVEN = (DEPTH + 1) // 2
N_ODD = DEPTH // 2
EPS = 1e-6

kernel_name = "hybrid_rglru_shortconv_pool_adaln"


def rmsnorm(x, g):
    xf = x.astype(jnp.float32)
    y = xf * lax.rsqrt(jnp.mean(xf * xf, axis=-1, keepdims=True) + EPS)
    return (y * g.astype(jnp.float32)).astype(x.dtype)


def causal_dwconv(x, w, b=None):
    k_width = w.shape[0]
    s = x.shape[1]
    xp = jnp.pad(x, ((0, 0), (k_width - 1, 0), (0, 0)))
    y = xp[:, 0:s] * w[0]
    for k in range(1, k_width):
        y = y + xp[:, k:k + s] * w[k]
    if b is not None:
        y = y + b
    return y


def rg_lru(x, w_a, b_a, w_x, b_x, lam):
    bsz, s, width = x.shape
    xh = x.reshape(bsz, s, LRU_HEADS, LRU_HEAD_DIM)
    r = jax.nn.sigmoid(jnp.einsum('bshi,hij->bshj', xh, w_a).reshape(bsz, s, width) + b_a)
    i = jax.nn.sigmoid(jnp.einsum('bshi,hij->bshj', xh, w_x).reshape(bsz, s, width) + b_x)
    log_a = -LRU_C * r.astype(jnp.float32) * jax.nn.softplus(-lam.astype(jnp.float32))
    a = jnp.exp(log_a)
    mult = jnp.sqrt(-jnp.expm1(2.0 * log_a))
    reset = (jnp.arange(s) == 0)[None, :, None]
    mult = jnp.where(reset, jnp.ones_like(mult), mult)
    u = mult * (i.astype(jnp.float32) * x.astype(jnp.float32))

    def combine(left, right):
        a_l, b_l = left
        a_r, b_r = right
        return a_l * a_r, a_r * b_l + b_r

    _, h = lax.associative_scan(combine, (a, u), axis=1)
    return h.astype(x.dtype)


def causal_window_mean(x, window):
    s = x.shape[1]
    cs = jnp.cumsum(x.astype(jnp.float32), axis=1)
    lag = jnp.pad(cs, ((0, 0), (window, 0), (0, 0)))[:, :s]
    count = jnp.minimum(jnp.arange(1, s + 1), window).astype(jnp.float32)[None, :, None]
    return ((cs - lag) / count).astype(x.dtype)


def even_mixer(h, w_in, conv_w, conv_b, w_a, b_a, w_x, b_x, lam, sc_conv_w, w_out):
    proj = h @ w_in
    xa, ga, gb_post, gc_pre, v, gb = jnp.split(proj, EVEN_SPLITS, axis=-1)
    ya = rg_lru(causal_dwconv(xa, conv_w, conv_b), w_a, b_a, w_x, b_x, lam)
    yb = gb_post * causal_dwconv(gc_pre * v, sc_conv_w)
    y = jnp.concatenate([ya * jax.nn.silu(ga), yb * jax.nn.silu(gb)], axis=-1)
    return y @ w_out


def odd_mixer(h, w_in, w_grp, b_grp, scale, w_out):
    bsz, s, _ = h.shape
    proj = h @ w_in
    v, g = jnp.split(proj, 2, axis=-1)
    vg = v.reshape(bsz, s, POOL_GROUPS, POOL_GROUP_DIM)
    pooled = jnp.stack([causal_window_mean(vg[:, :, k], POOL_WINDOWS[k])
                        for k in range(POOL_GROUPS)], axis=2)
    mixed = jnp.einsum('bsgc,gcd->bsgd', pooled - vg, w_grp).reshape(bsz, s, POOL_WIDTH) + b_grp
    y = mixed * scale * jax.nn.silu(g)
    return y @ w_out


def setup_inputs(seed: int = 0) -> dict:
    key = jax.random.key(seed)
    ks = jax.random.split(key, 21)
    f32 = jnp.float32
    nrm = lambda k, shape, s: jax.random.normal(k, shape, f32) * s
    d = D_MODEL
    u = jax.random.uniform(ks[12], (N_EVEN, LRU_WIDTH), f32, minval=0.9, maxval=0.999)
    return {
        "x": nrm(ks[0], (BATCH, SEQ, d), 1.0),
        "c": nrm(ks[1], (BATCH, d), 1.0),
        "norm_g": 1.0 + nrm(ks[2], (DEPTH, d), 0.05),
        "mod_w": nrm(ks[3], (DEPTH, d, 3 * d), 0.5 * d ** -0.5),
        "mod_b": nrm(ks[4], (DEPTH, 3 * d), 0.02),
        "hy_w_in": nrm(ks[5], (N_EVEN, d, EVEN_IN), d ** -0.5),
        "hy_conv_w": nrm(ks[6], (N_EVEN, LRU_CONV, LRU_WIDTH), LRU_CONV ** -0.5),
        "hy_conv_b": nrm(ks[7], (N_EVEN, LRU_WIDTH), 0.02),
        "lru_w_a": nrm(ks[8], (N_EVEN, LRU_HEADS, LRU_HEAD_DIM, LRU_HEAD_DIM), LRU_HEAD_DIM ** -0.5),
        "lru_b_a": nrm(ks[9], (N_EVEN, LRU_WIDTH), 0.02),
        "lru_w_x": nrm(ks[10], (N_EVEN, LRU_HEADS, LRU_HEAD_DIM, LRU_HEAD_DIM), LRU_HEAD_DIM ** -0.5),
        "lru_b_x": nrm(ks[11], (N_EVEN, LRU_WIDTH), 0.02),
        "lru_lambda": jnp.log(u) - jnp.log1p(-u),
        "sc_conv_w": nrm(ks[13], (N_EVEN, SC_CONV, SC_WIDTH), SC_CONV ** -0.5),
        "hy_w_out": nrm(ks[14], (N_EVEN, LRU_WIDTH + SC_WIDTH, d), (LRU_WIDTH + SC_WIDTH) ** -0.5),
        "pool_w_in": nrm(ks[15], (N_ODD, d, 2 * POOL_WIDTH), d ** -0.5),
        "pool_w_grp": nrm(ks[16], (N_ODD, POOL_GROUPS, POOL_GROUP_DIM, POOL_GROUP_DIM), POOL_GROUP_DIM ** -0.5),
        "pool_b_grp": nrm(ks[17], (N_ODD, POOL_WIDTH), 0.02),
        "pool_scale": 1.0 + nrm(ks[18], (N_ODD, POOL_WIDTH), 0.1),
        "pool_w_out": nrm(ks[19], (N_ODD, POOL_WIDTH, d), POOL_WIDTH ** -0.5),
        "final_g": 1.0 + nrm(ks[20], (d,), 0.05),
    }


def reference(x, c, norm_g, mod_w, mod_b, hy_w_in, hy_conv_w, hy_conv_b, lru_w_a, lru_b_a,
              lru_w_x, lru_b_x, lru_lambda, sc_conv_w, hy_w_out, pool_w_in, pool_w_grp,
              pool_b_grp, pool_scale, pool_w_out, final_g):
    c_act = jax.nn.silu(c)
    for layer in range(DEPTH):
        mod = c_act @ mod_w[layer] + mod_b[layer]
        shift, scale, gate = jnp.split(mod, 3, axis=-1)
        h = rmsnorm(x, norm_g[layer]) * (1.0 + scale[:, None, :]) + shift[:, None, :]
        if layer % 2 == 0:
            e = layer // 2
            y = even_mixer(h, hy_w_in[e], hy_conv_w[e], hy_conv_b[e], lru_w_a[e], lru_b_a[e],
                           lru_w_x[e], lru_b_x[e], lru_lambda[e], sc_conv_w[e], hy_w_out[e])
        else:
            o = layer // 2
            y = odd_mixer(h, pool_w_in[o], pool_w_grp[o], pool_b_grp[o], pool_scale[o], pool_w_out[o])
        x = x + gate[:, None, :] * y
    return rmsnorm(x, final_g)
```

```python
import functools

import jax
import jax.numpy as jnp
from jax import lax
from jax.experimental import pallas as pl
from jax.experimental.pallas import tpu as pltpu

EPS = 1e-6
LRU_C = 8.0
POOL_WINDOWS = (2, 4, 8, 16)

SUBLANES = 8
ROW_TILE = 256
MOD_COL_TILE = 512
VMEM_LIMIT_BYTES = 56 * 1024 * 1024

BF16 = jnp.bfloat16
F32 = jnp.float32


def _sigmoid(v):
    return 1.0 / (1.0 + jnp.exp(-v))


def _silu(v):
    return v * _sigmoid(v)


def _shift_rows(cur, prev, k, row):
    return jnp.where(row >= k, pltpu.roll(cur, k, 0), pltpu.roll(prev, k, 0))


def _mod_kernel(c_ref, w_ref, b_ref, o_ref):
    c = c_ref[...]
    c_act = _silu(c)
    o_ref[0] = jnp.sum(c_act * w_ref[0], axis=0, keepdims=True) + b_ref[0]


def _modulation(c, mod_w, mod_b):
    depth, d, n = mod_w.shape
    return pl.pallas_call(
        _mod_kernel,
        grid=(depth, n // MOD_COL_TILE),
        in_specs=[
            pl.BlockSpec((d, 1), lambda l, j: (0, 0)),
            pl.BlockSpec((1, d, MOD_COL_TILE), lambda l, j: (l, 0, j)),
            pl.BlockSpec((1, 1, MOD_COL_TILE), lambda l, j: (l, 0, j)),
        ],
        out_specs=pl.BlockSpec((1, 1, MOD_COL_TILE), lambda l, j: (l, 0, j)),
        out_shape=jax.ShapeDtypeStruct((depth, 1, n), F32),
        name="adaln_modulation",
    )(c.reshape(d, 1), mod_w, mod_b.reshape(depth, 1, n))


def _norm_modulate(x_ref, h_buf, gmul, shift, ts):
    rows = 2 * SUBLANES

    def body(r, carry):
        r0 = pl.multiple_of(r * rows, rows)
        xv = x_ref[pl.ds(r0, rows), :]
        ms = jnp.mean(xv * xv, axis=-1, keepdims=True)
        h_buf[pl.ds(r0, rows), :] = (xv * lax.rsqrt(ms + EPS) * gmul + shift).astype(BF16)
        return carry

    lax.fori_loop(0, ts // rows, body, 0)


def _const_spec(shape):
    zeros = (0,) * len(shape)
    return pl.BlockSpec(shape, lambda i: zeros, pipeline_mode=pl.Buffered(1))


def _even_kernel(x_ref, mod_ref, g_ref, win_ref, cw_ref, cb_ref, wg_ref, ba_ref, bx_ref, lam_ref,
                 scw_ref, wout_ref, o_ref,
                 h_buf, proj_buf, xc_buf, gate_buf, y_buf, xa_tail, cv_tail, h_carry):
    ts, d = x_ref.shape
    heads, hd, _ = wg_ref.shape
    step = pl.program_id(0)

    @pl.when(step == 0)
    def _():
        xa_tail[...] = jnp.zeros_like(xa_tail)
        cv_tail[...] = jnp.zeros_like(cv_tail)
        h_carry[...] = jnp.zeros_like(h_carry)

    shift = mod_ref[:, 0:d]
    scale = mod_ref[:, d:2 * d]
    gate = mod_ref[:, 2 * d:3 * d]
    gmul = g_ref[...] * (1.0 + scale)
    _norm_modulate(x_ref, h_buf, gmul, shift, ts)

    proj_buf[...] = jnp.dot(h_buf[...], win_ref[...], preferred_element_type=F32)

    row = lax.broadcasted_iota(jnp.int32, (SUBLANES, d), 0)

    def conv_body(g, prev):
        r0 = pl.multiple_of(g * SUBLANES, SUBLANES)
        cur = proj_buf[pl.ds(r0, SUBLANES), 0:d]
        acc = cur * cw_ref[3:4, :] + cb_ref[...]
        for k in (1, 2, 3):
            acc = acc + _shift_rows(cur, prev, k, row) * cw_ref[3 - k:4 - k, :]
        xc_buf[pl.ds(r0, SUBLANES), :] = acc
        return cur

    xa_tail[...] = lax.fori_loop(0, ts // SUBLANES, conv_body, xa_tail[...])

    for h in range(heads):
        lhs = xc_buf[:, h * hd:(h + 1) * hd].astype(BF16)
        res = jnp.dot(lhs, wg_ref[h], preferred_element_type=F32)
        gate_buf[:, h * hd:(h + 1) * hd] = res[:, 0:hd]
        gate_buf[:, d + h * hd:d + (h + 1) * hd] = res[:, hd:2 * hd]

    lam = lam_ref[...]
    neg_lam = -lam
    softplus = jnp.maximum(neg_lam, 0.0) + jnp.log1p(jnp.exp(-jnp.abs(neg_lam)))
    log_a_scale = -LRU_C * softplus
    first_tile = step == 0

    def scan_body(g, carry):
        h_prev, cv_prev = carry
        r0 = pl.multiple_of(g * SUBLANES, SUBLANES)
        rows = pl.ds(r0, SUBLANES)
        xc = xc_buf[rows, :]
        r_gate = _sigmoid(gate_buf[rows, 0:d] + ba_ref[...])
        i_gate = _sigmoid(gate_buf[rows, d:2 * d] + bx_ref[...])
        a = jnp.exp(r_gate * log_a_scale)
        mult = jnp.sqrt(1.0 - a * a)
        seq_start = jnp.logical_and(row == 0, jnp.logical_and(first_tile, g == 0))
        mult = jnp.where(seq_start, 1.0, mult)
        u = mult * (i_gate * xc)
        for s in (1, 2, 4):
            keep = row >= s
            u = a * jnp.where(keep, pltpu.roll(u, s, 0), 0.0) + u
            a = a * jnp.where(keep, pltpu.roll(a, s, 0), 1.0)
        hs = u + a * h_prev
        y_buf[rows, 0:d] = hs * _silu(proj_buf[rows, d:2 * d])

        cv = proj_buf[rows, 3 * d:4 * d] * proj_buf[rows, 4 * d:5 * d]
        acc = cv * scw_ref[2:3, :]
        for k in (1, 2):
            acc = acc + _shift_rows(cv, cv_prev, k, row) * scw_ref[2 - k:3 - k, :]
        y_buf[rows, d:2 * d] = proj_buf[rows, 2 * d:3 * d] * acc * _silu(proj_buf[rows, 5 * d:6 * d])
        h_last = jnp.broadcast_to(hs[SUBLANES - 1:SUBLANES, :], (SUBLANES, d))
        return h_last, cv

    h_last, cv_last = lax.fori_loop(0, ts // SUBLANES, scan_body, (h_carry[...], cv_tail[...]))
    h_carry[...] = h_last
    cv_tail[...] = cv_last

    y = jnp.dot(y_buf[...].astype(BF16), wout_ref[...], preferred_element_type=F32)
    o_ref[...] = x_ref[...] + gate * y


def _even_layer(x, mod, norm_g, w_in, conv_w, conv_b, w_gate, b_a, b_x, lam, sc_conv_w, w_out):
    seq, d = x.shape
    ts = ROW_TILE
    n_in = w_in.shape[1]
    row2 = lambda v: v.reshape(1, -1)
    return pl.pallas_call(
        _even_kernel,
        grid=(seq // ts,),
        in_specs=[
            pl.BlockSpec((ts, d), lambda i: (i, 0)),
            _const_spec((1, 3 * d)),
            _const_spec((1, d)),
            _const_spec(w_in.shape),
            _const_spec(conv_w.shape),
            _const_spec((1, d)),
            _const_spec(w_gate.shape),
            _const_spec((1, d)),
            _const_spec((1, d)),
            _const_spec((1, d)),
            _const_spec(sc_conv_w.shape),
            _const_spec(w_out.shape),
        ],
        out_specs=pl.BlockSpec((ts, d), lambda i: (i, 0)),
        out_shape=jax.ShapeDtypeStruct((seq, d), F32),
        scratch_shapes=[
            pltpu.VMEM((ts, d), BF16),
            pltpu.VMEM((ts, n_in), F32),
            pltpu.VMEM((ts, d), F32),
            pltpu.VMEM((ts, 2 * d), F32),
            pltpu.VMEM((ts, 2 * d), F32),
            pltpu.VMEM((SUBLANES, d), F32),
            pltpu.VMEM((SUBLANES, d), F32),
            pltpu.VMEM((SUBLANES, d), F32),
        ],
        compiler_params=pltpu.CompilerParams(
            dimension_semantics=("arbitrary",), vmem_limit_bytes=VMEM_LIMIT_BYTES),
        name="even_layer",
    )(x, mod, row2(norm_g), w_in, conv_w, row2(conv_b), w_gate, row2(b_a), row2(b_x), row2(lam),
      sc_conv_w, w_out)


def _odd_kernel(x_ref, mod_ref, g_ref, win_ref, wgrp_ref, bgrp_ref, pscale_ref, wout_ref, fg_ref, o_ref,
                h_buf, proj_buf, diff_buf, mixed_buf, *tails):
    ts, d = x_ref.shape
    n_groups, gd, _ = wgrp_ref.shape
    width = n_groups * gd
    step = pl.program_id(0)

    @pl.when(step == 0)
    def _():
        for t in tails:
            t[...] = jnp.zeros_like(t)

    shift = mod_ref[:, 0:d]
    scale = mod_ref[:, d:2 * d]
    gate = mod_ref[:, 2 * d:3 * d]
    gmul = g_ref[...] * (1.0 + scale)
    _norm_modulate(x_ref, h_buf, gmul, shift, ts)

    proj_buf[...] = jnp.dot(h_buf[...], win_ref[...], preferred_element_type=F32)

    row = lax.broadcasted_iota(jnp.int32, (SUBLANES, gd), 0)
    tile_start = step * ts

    tail_at = 0
    for k, window in enumerate(POOL_WINDOWS):
        levels = window.bit_length() - 1
        grp_tails = tails[tail_at:tail_at + levels]
        tail_at += levels
        cols = slice(k * gd, (k + 1) * gd)

        def pool_body(g, prevs, cols=cols, window=window, levels=levels):
            r0 = pl.multiple_of(g * SUBLANES, SUBLANES)
            rows = pl.ds(r0, SUBLANES)
            v = proj_buf[rows, cols]
            cur = v
            new_prevs = []
            for lvl in range(levels):
                new_prevs.append(cur)
                s = 1 << lvl
                if s < SUBLANES:
                    cur = cur + _shift_rows(cur, prevs[lvl], s, row)
                else:
                    cur = cur + prevs[lvl]
            t = tile_start + r0 + row
            count = jnp.minimum(t + 1, window).astype(F32)
            diff_buf[rows, cols] = cur / count - v
            return tuple(new_prevs)

        last = lax.fori_loop(0, ts // SUBLANES, pool_body, tuple(t[...] for t in grp_tails))
        for t, val in zip(grp_tails, last):
            t[...] = val

    for k in range(n_groups):
        cols = slice(k * gd, (k + 1) * gd)
        mixed_buf[:, cols] = jnp.dot(diff_buf[:, cols].astype(BF16), wgrp_ref[k],
                                     preferred_element_type=F32)

    y = (mixed_buf[...] + bgrp_ref[...]) * pscale_ref[...] * _silu(proj_buf[:, width:2 * width])
    out = jnp.dot(y.astype(BF16), wout_ref[...], preferred_element_type=F32)
    x2 = x_ref[...] + gate * out
    ms = jnp.mean(x2 * x2, axis=-1, keepdims=True)
    o_ref[...] = x2 * lax.rsqrt(ms + EPS) * fg_ref[...]


def _odd_layer(x, mod, norm_g, w_in, w_grp, b_grp, p_scale, w_out, final_g):
    seq, d = x.shape
    ts = ROW_TILE
    n_groups, gd, _ = w_grp.shape
    width = n_groups * gd
    row2 = lambda v: v.reshape(1, -1)
    n_tails = sum(w.bit_length() - 1 for w in POOL_WINDOWS)
    return pl.pallas_call(
        _odd_kernel,
        grid=(seq // ts,),
        in_specs=[
            pl.BlockSpec((ts, d), lambda i: (i, 0)),
            _const_spec((1, 3 * d)),
            _const_spec((1, d)),
            _const_spec(w_in.shape),
            _const_spec(w_grp.shape),
            _const_spec((1, width)),
            _const_spec((1, width)),
            _const_spec(w_out.shape),
            _const_spec((1, d)),
        ],
        out_specs=pl.BlockSpec((ts, d), lambda i: (i, 0)),
        out_shape=jax.ShapeDtypeStruct((seq, d), F32),
        scratch_shapes=[
            pltpu.VMEM((ts, d), BF16),
            pltpu.VMEM((ts, 2 * width), F32),
            pltpu.VMEM((ts, width), F32),
            pltpu.VMEM((ts, width), F32),
        ] + [pltpu.VMEM((SUBLANES, gd), F32)] * n_tails,
        compiler_params=pltpu.CompilerParams(
            dimension_semantics=("arbitrary",), vmem_limit_bytes=VMEM_LIMIT_BYTES),
        name="odd_layer",
    )(x, mod, row2(norm_g), w_in, w_grp, row2(b_grp), row2(p_scale), w_out, row2(final_g))


def kernel(x, c, norm_g, mod_w, mod_b, hy_w_in, hy_conv_w, hy_conv_b, lru_w_a, lru_b_a, lru_w_x, lru_b_x, lru_lambda, sc_conv_w, hy_w_out, pool_w_in, pool_w_grp, pool_b_grp, pool_scale, pool_w_out, final_g):
    batch, seq, d = x.shape
    assert batch == 1 and mod_w.shape[0] == 2 and seq % ROW_TILE == 0
    mods = _modulation(c, mod_w, mod_b)
    w_gate = jnp.concatenate([lru_w_a[0], lru_w_x[0]], axis=-1).astype(BF16)
    x1 = _even_layer(x[0], mods[0], norm_g[0], hy_w_in[0].astype(BF16), hy_conv_w[0], hy_conv_b[0],
                     w_gate, lru_b_a[0], lru_b_x[0], lru_lambda[0], sc_conv_w[0],
                     hy_w_out[0].astype(BF16))
    out = _odd_layer(x1, mods[1], norm_g[1], pool_w_in[0].astype(BF16), pool_w_grp[0].astype(BF16),
                     pool_b_grp[0], pool_scale[0], pool_w_out[0].astype(BF16), final_g)
    return out[None]
```

```python
import jax
import jax.numpy as jnp
from jax import lax
from jax.experimental import pallas as pl
from jax.experimental.pallas import tpu as pltpu

EPS = 1e-6
LRU_C = 8.0
POOL_WINDOWS = (2, 4, 8, 16)

SUBLANES = 8
ROW_TILE = 256
TILES_PER_STEP = 2
NORM_ROWS = 16
MOD_COL_TILE = 512
VMEM_LIMIT_BYTES = 56 * 1024 * 1024

BF16 = jnp.bfloat16
F32 = jnp.float32


def _sigmoid(v):
    return 1.0 / (1.0 + jnp.exp(-v))


def _silu(v):
    return v * _sigmoid(v)


def _bcast_rows(ref_row, rows=SUBLANES):
    return jnp.broadcast_to(ref_row, (rows, ref_row.shape[-1]))


def _mod_kernel(c_ref, w_ref, b_ref, o_ref):
    c = c_ref[...]
    c_act = _silu(c)
    o_ref[0] = jnp.sum(c_act * w_ref[0], axis=0, keepdims=True) + b_ref[0]


def _modulation(c, mod_w, mod_b):
    depth, d, n = mod_w.shape
    return pl.pallas_call(
        _mod_kernel,
        grid=(depth, n // MOD_COL_TILE),
        in_specs=[
            pl.BlockSpec((d, 1), lambda l, j: (0, 0)),
            pl.BlockSpec((1, d, MOD_COL_TILE), lambda l, j: (l, 0, j)),
            pl.BlockSpec((1, 1, MOD_COL_TILE), lambda l, j: (l, 0, j)),
        ],
        out_specs=pl.BlockSpec((1, 1, MOD_COL_TILE), lambda l, j: (l, 0, j)),
        out_shape=jax.ShapeDtypeStruct((depth, 1, n), F32),
        name="adaln_modulation",
    )(c.reshape(d, 1), mod_w, mod_b.reshape(depth, 1, n))


def _norm_modulate(x_ref, row0, h_buf, gmul, shift, ts):
    for r in range(ts // NORM_ROWS):
        xv = x_ref[row0 + r * NORM_ROWS:row0 + (r + 1) * NORM_ROWS, :]
        ms = jnp.mean(xv * xv, axis=-1, keepdims=True)
        h_buf[r * NORM_ROWS:(r + 1) * NORM_ROWS, :] = (
            xv * lax.rsqrt(ms + EPS) * gmul + shift).astype(BF16)


def _const_spec(shape):
    zeros = (0,) * len(shape)
    return pl.BlockSpec(shape, lambda i: zeros, pipeline_mode=pl.Buffered(1))


def _tile_specs(ts, d, n_tiles):
    cur = pl.BlockSpec((TILES_PER_STEP * ts, d), lambda i: (i, 0))
    nxt = pl.BlockSpec((ts, d), lambda i: (jnp.minimum(TILES_PER_STEP * (i + 1), n_tiles - 1), 0))
    return cur, nxt


def _even_kernel(x_ref, xn_ref, mod_ref, g_ref, win_ref, cw_ref, cb_ref, wg_ref, ba_ref, bx_ref,
                 lam_ref, scw_ref, wout_ref, o_ref,
                 h_buf, proj_a, proj_b, xc_buf, gate_buf, y_buf, xa_tail, cv_tail, h_carry):
    ts, d = xn_ref.shape
    heads, hd, _ = wg_ref.shape
    n_groups = ts // SUBLANES
    step = pl.program_id(0)
    first_step = step == 0

    shift = mod_ref[:, 0:d]
    scale = mod_ref[:, d:2 * d]
    gate = mod_ref[:, 2 * d:3 * d]
    gmul_n = _bcast_rows(g_ref[...] * (1.0 + scale), NORM_ROWS)
    shift_n = _bcast_rows(shift, NORM_ROWS)

    @pl.when(first_step)
    def _():
        xa_tail[...] = jnp.zeros_like(xa_tail)
        cv_tail[...] = jnp.zeros_like(cv_tail)
        h_carry[...] = jnp.zeros_like(h_carry)
        _norm_modulate(x_ref, 0, h_buf, gmul_n, shift_n, ts)
        proj_a[...] = jnp.dot(h_buf[...], win_ref[...], preferred_element_type=F32)

    row = lax.broadcasted_iota(jnp.int32, (SUBLANES, d), 0)
    row0 = row == 0
    cw = [_bcast_rows(cw_ref[k:k + 1, :]) for k in range(4)]
    cb = _bcast_rows(cb_ref[...])
    scw = [_bcast_rows(scw_ref[k:k + 1, :]) for k in range(3)]
    ba = _bcast_rows(ba_ref[...])
    bx = _bcast_rows(bx_ref[...])
    neg_lam = -lam_ref[...]
    softplus = jnp.maximum(neg_lam, 0.0) + jnp.log1p(jnp.exp(-jnp.abs(neg_lam)))
    log_a_scale = _bcast_rows(-LRU_C * softplus)

    def shifted(rolled_cur, rolled_prev, k):
        return jnp.where(row >= k, rolled_cur, rolled_prev)

    def tile(x_row0, next_ref, next_row0, proj, proj_next, seq_start):
        _norm_modulate(next_ref, next_row0, h_buf, gmul_n, shift_n, ts)
        proj_next[...] = jnp.dot(h_buf[...], win_ref[...], preferred_element_type=F32)

        prev = xa_tail[...]
        prev_rolls = [pltpu.roll(prev, k, 0) for k in (1, 2, 3)]
        for g in range(n_groups):
            rows = slice(g * SUBLANES, (g + 1) * SUBLANES)
            cur = proj[rows, 0:d]
            rolls = [pltpu.roll(cur, k, 0) for k in (1, 2, 3)]
            acc = cur * cw[3] + cb
            for k in (1, 2, 3):
                acc = acc + shifted(rolls[k - 1], prev_rolls[k - 1], k) * cw[3 - k]
            xc_buf[rows, :] = acc
            prev_rolls = rolls
        xa_tail[...] = proj[ts - SUBLANES:ts, 0:d]

        for h in range(heads):
            lhs = xc_buf[:, h * hd:(h + 1) * hd].astype(BF16)
            res = jnp.dot(lhs, wg_ref[h], preferred_element_type=F32)
            gate_buf[:, h * hd:(h + 1) * hd] = res[:, 0:hd]
            gate_buf[:, d + h * hd:d + (h + 1) * hd] = res[:, hd:2 * hd]

        hs = h_carry[...]
        cv_prev = cv_tail[...]
        cv_prev_rolls = [pltpu.roll(cv_prev, k, 0) for k in (1, 2)]
        for g in range(n_groups):
            rows = slice(g * SUBLANES, (g + 1) * SUBLANES)
            xc = xc_buf[rows, :]
            r_gate = _sigmoid(gate_buf[rows, 0:d] + ba)
            i_gate = _sigmoid(gate_buf[rows, d:2 * d] + bx)
            a = jnp.exp(r_gate * log_a_scale)
            z = 1.0 - a * a
            mult = jnp.where(z > 0.0, z * lax.rsqrt(z), 0.0)
            if g == 0 and seq_start is not None:
                mult = jnp.where(jnp.logical_and(row0, seq_start), 1.0, mult)
            u = mult * (i_gate * xc)
            u = u + jnp.where(row0, a * pltpu.roll(hs, 1, 0), 0.0)
            a = jnp.where(row0, 0.0, a)
            u = u + a * pltpu.roll(u, 1, 0)
            a = a * pltpu.roll(a, 1, 0)
            u = u + a * pltpu.roll(u, 2, 0)
            a = a * pltpu.roll(a, 2, 0)
            hs = u + a * pltpu.roll(u, 4, 0)
            y_buf[rows, 0:d] = hs * _silu(proj[rows, d:2 * d])

            cv = proj[rows, 3 * d:4 * d] * proj[rows, 4 * d:5 * d]
            cv_rolls = [pltpu.roll(cv, k, 0) for k in (1, 2)]
            acc = cv * scw[2]
            for k in (1, 2):
                acc = acc + shifted(cv_rolls[k - 1], cv_prev_rolls[k - 1], k) * scw[2 - k]
            y_buf[rows, d:2 * d] = proj[rows, 2 * d:3 * d] * acc * _silu(proj[rows, 5 * d:6 * d])
            cv_prev_rolls = cv_rolls
            if g == n_groups - 1:
                cv_tail[...] = cv
        h_carry[...] = hs

        y = jnp.dot(y_buf[...].astype(BF16), wout_ref[...], preferred_element_type=F32)
        o_ref[x_row0:x_row0 + ts, :] = x_ref[x_row0:x_row0 + ts, :] + gate * y

    tile(0, x_ref, ts, proj_a, proj_b, first_step)
    tile(ts, xn_ref, 0, proj_b, proj_a, None)


def _even_layer(x, mod, norm_g, w_in, conv_w, conv_b, w_gate, b_a, b_x, lam, sc_conv_w, w_out):
    seq, d = x.shape
    ts = ROW_TILE
    n_in = w_in.shape[1]
    n_tiles = seq // ts
    row2 = lambda v: v.reshape(1, -1)
    x_cur, x_next = _tile_specs(ts, d, n_tiles)
    return pl.pallas_call(
        _even_kernel,
        grid=(n_tiles // TILES_PER_STEP,),
        in_specs=[
            x_cur, x_next,
            _const_spec((1, 3 * d)),
            _const_spec((1, d)),
            _const_spec(w_in.shape),
            _const_spec(conv_w.shape),
            _const_spec((1, d)),
            _const_spec(w_gate.shape),
            _const_spec((1, d)),
            _const_spec((1, d)),
            _const_spec((1, d)),
            _const_spec(sc_conv_w.shape),
            _const_spec(w_out.shape),
        ],
        out_specs=pl.BlockSpec((TILES_PER_STEP * ts, d), lambda i: (i, 0)),
        out_shape=jax.ShapeDtypeStruct((seq, d), F32),
        scratch_shapes=[
            pltpu.VMEM((ts, d), BF16),
            pltpu.VMEM((ts, n_in), F32),
            pltpu.VMEM((ts, n_in), F32),
            pltpu.VMEM((ts, d), F32),
            pltpu.VMEM((ts, 2 * d), F32),
            pltpu.VMEM((ts, 2 * d), F32),
            pltpu.VMEM((SUBLANES, d), F32),
            pltpu.VMEM((SUBLANES, d), F32),
            pltpu.VMEM((SUBLANES, d), F32),
        ],
        compiler_params=pltpu.CompilerParams(
            dimension_semantics=("arbitrary",), vmem_limit_bytes=VMEM_LIMIT_BYTES),
        name="even_layer",
    )(x, x, mod, row2(norm_g), w_in, conv_w, row2(conv_b), w_gate, row2(b_a), row2(b_x), row2(lam),
      sc_conv_w, w_out)


def _odd_kernel(x_ref, xn_ref, mod_ref, g_ref, win_ref, wgrp_ref, bgrp_ref, pscale_ref, wout_ref,
                fg_ref, o_ref,
                h_buf, proj_a, proj_b, diff_buf, mixed_buf, tails):
    ts, d = xn_ref.shape
    n_pool, gd, _ = wgrp_ref.shape
    width = n_pool * gd
    n_groups = ts // SUBLANES
    step = pl.program_id(0)
    first_step = step == 0

    shift = mod_ref[:, 0:d]
    scale = mod_ref[:, d:2 * d]
    gate = mod_ref[:, 2 * d:3 * d]
    gmul_n = _bcast_rows(g_ref[...] * (1.0 + scale), NORM_ROWS)
    shift_n = _bcast_rows(shift, NORM_ROWS)

    @pl.when(first_step)
    def _():
        tails[...] = jnp.zeros_like(tails)
        _norm_modulate(x_ref, 0, h_buf, gmul_n, shift_n, ts)
        proj_a[...] = jnp.dot(h_buf[...], win_ref[...], preferred_element_type=F32)

    row = lax.broadcasted_iota(jnp.int32, (SUBLANES, gd), 0)

    def tile(x_row0, next_ref, next_row0, proj, proj_next, seq_start):
        _norm_modulate(next_ref, next_row0, h_buf, gmul_n, shift_n, ts)
        proj_next[...] = jnp.dot(h_buf[...], win_ref[...], preferred_element_type=F32)

        tail_at = 0
        for k, window in enumerate(POOL_WINDOWS):
            levels = window.bit_length() - 1
            cols = slice(k * gd, (k + 1) * gd)
            tail_rows = [slice((tail_at + l) * SUBLANES, (tail_at + l + 1) * SUBLANES)
                         for l in range(levels)]
            tail_at += levels
            prev = []
            for l in range(levels):
                t = tails[tail_rows[l], :]
                prev.append(pltpu.roll(t, 1 << l, 0) if (1 << l) < SUBLANES else t)
            for g in range(n_groups):
                rows = slice(g * SUBLANES, (g + 1) * SUBLANES)
                v = proj[rows, cols]
                cur = v
                for l in range(levels):
                    s = 1 << l
                    if g == n_groups - 1:
                        tails[tail_rows[l], :] = cur
                    if s < SUBLANES:
                        rolled = pltpu.roll(cur, s, 0)
                        nxt = cur + jnp.where(row >= s, rolled, prev[l])
                        prev[l] = rolled
                    else:
                        nxt = cur + prev[l]
                        prev[l] = cur
                    cur = nxt
                inv = 1.0 / window
                if seq_start is not None and g * SUBLANES < window - 1:
                    count = jnp.minimum(row + (g * SUBLANES + 1), window).astype(F32)
                    inv = jnp.where(seq_start, 1.0 / count, inv)
                diff_buf[rows, cols] = cur * inv - v

        for k in range(n_pool):
            cols = slice(k * gd, (k + 1) * gd)
            mixed_buf[:, cols] = jnp.dot(diff_buf[:, cols].astype(BF16), wgrp_ref[k],
                                         preferred_element_type=F32)

        y = (mixed_buf[...] + bgrp_ref[...]) * pscale_ref[...] * _silu(proj[:, width:2 * width])
        out = jnp.dot(y.astype(BF16), wout_ref[...], preferred_element_type=F32)
        x2 = x_ref[x_row0:x_row0 + ts, :] + gate * out
        ms = jnp.mean(x2 * x2, axis=-1, keepdims=True)
        o_ref[x_row0:x_row0 + ts, :] = x2 * lax.rsqrt(ms + EPS) * fg_ref[...]

    tile(0, x_ref, ts, proj_a, proj_b, first_step)
    tile(ts, xn_ref, 0, proj_b, proj_a, None)


def _odd_layer(x, mod, norm_g, w_in, w_grp, b_grp, p_scale, w_out, final_g):
    seq, d = x.shape
    ts = ROW_TILE
    n_pool, gd, _ = w_grp.shape
    width = n_pool * gd
    n_tiles = seq // ts
    row2 = lambda v: v.reshape(1, -1)
    n_tails = sum(w.bit_length() - 1 for w in POOL_WINDOWS)
    x_cur, x_next = _tile_specs(ts, d, n_tiles)
    return pl.pallas_call(
        _odd_kernel,
        grid=(n_tiles // TILES_PER_STEP,),
        in_specs=[
            x_cur, x_next,
            _const_spec((1, 3 * d)),
            _const_spec((1, d)),
            _const_spec(w_in.shape),
            _const_spec(w_grp.shape),
            _const_spec((1, width)),
            _const_spec((1, width)),
            _const_spec(w_out.shape),
            _const_spec((1, d)),
        ],
        out_specs=pl.BlockSpec((TILES_PER_STEP * ts, d), lambda i: (i, 0)),
        out_shape=jax.ShapeDtypeStruct((seq, d), F32),
        scratch_shapes=[
            pltpu.VMEM((ts, d), BF16),
            pltpu.VMEM((ts, 2 * width), F32),
            pltpu.VMEM((ts, 2 * width), F32),
            pltpu.VMEM((ts, width), F32),
            pltpu.VMEM((ts, width), F32),
            pltpu.VMEM((n_tails * SUBLANES, gd), F32),
        ],
        compiler_params=pltpu.CompilerParams(
            dimension_semantics=("arbitrary",), vmem_limit_bytes=VMEM_LIMIT_BYTES),
        name="odd_layer",
    )(x, x, mod, row2(norm_g), w_in, w_grp, row2(b_grp), row2(p_scale), w_out, row2(final_g))


def kernel(x, c, norm_g, mod_w, mod_b, hy_w_in, hy_conv_w, hy_conv_b, lru_w_a, lru_b_a, lru_w_x, lru_b_x, lru_lambda, sc_conv_w, hy_w_out, pool_w_in, pool_w_grp, pool_b_grp, pool_scale, pool_w_out, final_g):
    batch, seq, d = x.shape
    assert batch == 1 and mod_w.shape[0] == 2 and seq % (ROW_TILE * TILES_PER_STEP) == 0
    mods = _modulation(c, mod_w, mod_b)
    w_gate = jnp.concatenate([lru_w_a[0], lru_w_x[0]], axis=-1).astype(BF16)
    x1 = _even_layer(x[0], mods[0], norm_g[0], hy_w_in[0].astype(BF16), hy_conv_w[0], hy_conv_b[0],
                     w_gate, lru_b_a[0], lru_b_x[0], lru_lambda[0], sc_conv_w[0],
                     hy_w_out[0].astype(BF16))
    out = _odd_layer(x1, mods[1], norm_g[1], pool_w_in[0].astype(BF16), pool_w_grp[0].astype(BF16),
                     pool_b_grp[0], pool_scale[0], pool_w_out[0].astype(BF16), final_g)
    return out[None]
```

```python
import jax
import jax.numpy as jnp
from jax import lax
from jax.experimental import pallas as pl
from jax.experimental.pallas import tpu as pltpu

EPS = 1e-6
LRU_C = 8.0
POOL_WINDOWS = (2, 4, 8, 16)

SUBLANES = 8
PACKED_ROWS = 16
ROW_TILE = 256
TILES_PER_STEP = 2
N_PIECE = 512
MOD_COL_TILE = 512
VMEM_LIMIT_BYTES = 56 * 1024 * 1024

BF16 = jnp.bfloat16
F32 = jnp.float32


NEG_LOG2E = -1.4426950408889634


def _sigmoid(v):
    return 1.0 / (1.0 + jnp.exp2(v * NEG_LOG2E))


def _silu(v):
    return v * _sigmoid(v)


def _bcast_rows(ref_row, rows=SUBLANES):
    return jnp.broadcast_to(ref_row, (rows, ref_row.shape[-1]))


def _interleave(mxu_items, valu_items):
    total_m = float(sum(c for c, _ in mxu_items)) or 1.0
    total_v = float(sum(c for c, _ in valu_items)) or 1.0
    im = iv = 0
    done_m = done_v = 0.0
    while im < len(mxu_items) or iv < len(valu_items):
        take_m = iv >= len(valu_items) or (
            im < len(mxu_items) and done_m / total_m <= done_v / total_v)
        if take_m:
            cost, thunk = mxu_items[im]
            im += 1
            done_m += cost
        else:
            cost, thunk = valu_items[iv]
            iv += 1
            done_v += cost
        thunk()


def _mod_kernel(c_ref, w_ref, b_ref, o_ref):
    c = c_ref[...]
    c_act = _silu(c)
    o_ref[0] = jnp.sum(c_act * w_ref[0], axis=0, keepdims=True) + b_ref[0]


def _modulation(c, mod_w, mod_b):
    depth, d, n = mod_w.shape
    return pl.pallas_call(
        _mod_kernel,
        grid=(depth, n // MOD_COL_TILE),
        in_specs=[
            pl.BlockSpec((d, 1), lambda l, j: (0, 0)),
            pl.BlockSpec((1, d, MOD_COL_TILE), lambda l, j: (l, 0, j)),
            pl.BlockSpec((1, 1, MOD_COL_TILE), lambda l, j: (l, 0, j)),
        ],
        out_specs=pl.BlockSpec((1, 1, MOD_COL_TILE), lambda l, j: (l, 0, j)),
        out_shape=jax.ShapeDtypeStruct((depth, 1, n), F32),
        name="adaln_modulation",
    )(c.reshape(d, 1), mod_w, mod_b.reshape(depth, 1, n))


def _norm_chunks(x_ref, row0, h_buf, gmul, shift, ts):
    def chunk(r):
        def run():
            xv = x_ref[row0 + r * PACKED_ROWS:row0 + (r + 1) * PACKED_ROWS, :]
            ms = jnp.mean(xv * xv, axis=-1, keepdims=True)
            h_buf[r * PACKED_ROWS:(r + 1) * PACKED_ROWS, :] = (
                xv * lax.rsqrt(ms + EPS) * gmul + shift).astype(BF16)
        return (25, run)
    return [chunk(r) for r in range(ts // PACKED_ROWS)]


def _dot_pieces(dst_ref, lhs_ref, w_ref, cost):
    def piece(p):
        cols = slice(p * N_PIECE, (p + 1) * N_PIECE)
        def run():
            dst_ref[:, cols] = jnp.dot(lhs_ref[...], w_ref[:, cols], preferred_element_type=F32)
        return (cost, run)
    return [piece(p) for p in range(w_ref.shape[1] // N_PIECE)]


def _run(items):
    for _, thunk in items:
        thunk()


def _const_spec(shape):
    zeros = (0,) * len(shape)
    return pl.BlockSpec(shape, lambda i: zeros, pipeline_mode=pl.Buffered(1))


def _tile_specs(ts, d, n_steps):
    rows = TILES_PER_STEP * ts
    cur = pl.BlockSpec((rows, d), lambda i: (i, 0))
    nxt = pl.BlockSpec((rows, d), lambda i: (jnp.minimum(i + 1, n_steps - 1), 0))
    return cur, nxt


def _even_kernel(x_ref, xn_ref, mod_ref, g_ref, win_ref, cw_ref, cb_ref, wg_ref, ba_ref, bx_ref,
                 lam_ref, scw_ref, wout_ref, o_ref,
                 h_a, h_b, proj_a, proj_b, xc_buf, gate_buf, y_buf, xa_tail, cv_tail, h_carry):
    d = x_ref.shape[1]
    ts = proj_a.shape[0]
    heads, hd, _ = wg_ref.shape
    n_groups = ts // SUBLANES
    first_step = pl.program_id(0) == 0

    shift = mod_ref[:, 0:d]
    scale = mod_ref[:, d:2 * d]
    gate = mod_ref[:, 2 * d:3 * d]
    gmul_n = _bcast_rows(g_ref[...] * (1.0 + scale), PACKED_ROWS)
    shift_n = _bcast_rows(shift, PACKED_ROWS)

    row = lax.broadcasted_iota(jnp.int32, (SUBLANES, d), 0)
    row0 = row == 0
    cw = [_bcast_rows(cw_ref[k:k + 1, :]) for k in range(4)]
    cb = _bcast_rows(cb_ref[...])

    def shifted(rolled_cur, rolled_prev, k):
        return jnp.where(row >= k, rolled_cur, rolled_prev)

    def conv_groups(proj):
        state = {}

        def group(g):
            def run():
                if g == 0:
                    state["rolls"] = [pltpu.roll(xa_tail[...], k, 0) for k in (1, 2, 3)]
                rows = slice(g * SUBLANES, (g + 1) * SUBLANES)
                cur = proj[rows, 0:d]
                rolls = [pltpu.roll(cur, k, 0) for k in (1, 2, 3)]
                acc = cur * cw[3] + cb
                for k in (1, 2, 3):
                    acc = acc + shifted(rolls[k - 1], state["rolls"][k - 1], k) * cw[3 - k]
                xc_buf[rows, :] = acc
                state["rolls"] = rolls
                if g == n_groups - 1:
                    xa_tail[...] = cur
            return (40, run)

        return [group(g) for g in range(n_groups)]

    @pl.when(first_step)
    def _():
        xa_tail[...] = jnp.zeros_like(xa_tail)
        cv_tail[...] = jnp.zeros_like(cv_tail)
        h_carry[...] = jnp.zeros_like(h_carry)
        _run(_norm_chunks(x_ref, 0, h_b, gmul_n, shift_n, ts))
        proj_a[...] = jnp.dot(h_b[...], win_ref[...], preferred_element_type=F32)
        _run(_norm_chunks(x_ref, ts, h_a, gmul_n, shift_n, ts))
        _run(conv_groups(proj_a))

    scw = [_bcast_rows(scw_ref[k:k + 1, :]) for k in range(3)]
    ba = _bcast_rows(ba_ref[...])
    bx = _bcast_rows(bx_ref[...])
    neg_lam = -lam_ref[...]
    softplus = jnp.maximum(neg_lam, 0.0) + jnp.log1p(jnp.exp(-jnp.abs(neg_lam)))
    log2_a_scale = _bcast_rows((LRU_C * NEG_LOG2E) * softplus)

    def tile(x_row0, proj, proj_next, h_cur, h_next, next_row0, seq_start):
        in_pieces = _dot_pieces(proj_next, h_cur, win_ref, 560)
        state = {}

        def gate_head(h):
            def run():
                lhs = xc_buf[:, h * hd:(h + 1) * hd].astype(BF16)
                res = jnp.dot(lhs, wg_ref[h], preferred_element_type=F32)
                gate_buf[:, h * hd:(h + 1) * hd] = res[:, 0:hd]
                gate_buf[:, d + h * hd:d + (h + 1) * hd] = res[:, hd:2 * hd]
            return (40, run)

        state["cv_rolls"] = [pltpu.roll(cv_tail[...], k, 0) for k in (1, 2)]

        def conv_b_unit(m):
            def run():
                out = []
                for g in (2 * m, 2 * m + 1):
                    rows = slice(g * SUBLANES, (g + 1) * SUBLANES)
                    cv = proj[rows, 3 * d:4 * d] * proj[rows, 4 * d:5 * d]
                    rolls = [pltpu.roll(cv, k, 0) for k in (1, 2)]
                    acc = cv * scw[2]
                    for k in (1, 2):
                        acc = acc + shifted(rolls[k - 1], state["cv_rolls"][k - 1], k) * scw[2 - k]
                    out.append(proj[rows, 2 * d:3 * d] * acc * _silu(proj[rows, 5 * d:6 * d]))
                    state["cv_rolls"] = rolls
                    if g == n_groups - 1:
                        cv_tail[...] = cv
                y_buf[m * PACKED_ROWS:(m + 1) * PACKED_ROWS, d:2 * d] = (
                    jnp.concatenate(out, axis=0).astype(BF16))
            return (110, run)

        state["hs"] = h_carry[...]

        def lru_unit(m):
            def run():
                out = []
                for g in (2 * m, 2 * m + 1):
                    rows = slice(g * SUBLANES, (g + 1) * SUBLANES)
                    xc = xc_buf[rows, :]
                    r_gate = _sigmoid(gate_buf[rows, 0:d] + ba)
                    i_gate = _sigmoid(gate_buf[rows, d:2 * d] + bx)
                    a = jnp.exp2(r_gate * log2_a_scale)
                    z = 1.0 - a * a
                    mult = jnp.where(z > 0.0, z * lax.rsqrt(z), 0.0)
                    if g == 0 and seq_start is not None:
                        mult = jnp.where(jnp.logical_and(row0, seq_start), 1.0, mult)
                    u = mult * (i_gate * xc)
                    u = u + jnp.where(row0, a * pltpu.roll(state["hs"], 1, 0), 0.0)
                    a = jnp.where(row0, 0.0, a)
                    u = u + a * pltpu.roll(u, 1, 0)
                    a = a * pltpu.roll(a, 1, 0)
                    u = u + a * pltpu.roll(u, 2, 0)
                    a = a * pltpu.roll(a, 2, 0)
                    hs = u + a * pltpu.roll(u, 4, 0)
                    state["hs"] = hs
                    out.append(hs * _silu(proj[rows, d:2 * d]))
                    if g == n_groups - 1:
                        h_carry[...] = hs
                y_buf[m * PACKED_ROWS:(m + 1) * PACKED_ROWS, 0:d] = (
                    jnp.concatenate(out, axis=0).astype(BF16))
            return (220, run)

        n_units = n_groups // 2
        conv_b_units = [conv_b_unit(m) for m in range(n_units)]
        lru_units = [lru_unit(m) for m in range(n_units)]
        gate_heads = [gate_head(h) for h in range(heads)]

        norm_chunks = _norm_chunks(xn_ref, next_row0, h_next, gmul_n, shift_n, ts)
        _interleave(gate_heads + in_pieces[:4], conv_b_units + norm_chunks)
        _interleave(in_pieces[4:], lru_units)

        def out_piece(p):
            cols = slice(p * N_PIECE, (p + 1) * N_PIECE)
            rows = slice(x_row0, x_row0 + ts)
            def run():
                y = jnp.dot(y_buf[...], wout_ref[:, cols], preferred_element_type=F32)
                o_ref[rows, cols] = x_ref[rows, cols] + gate[:, cols] * y
            return (1100, run)

        _interleave([out_piece(p) for p in range(d // N_PIECE)], conv_groups(proj_next))

    tile(0, proj_a, proj_b, h_a, h_b, 0, first_step)
    tile(ts, proj_b, proj_a, h_b, h_a, ts, None)


def _even_layer(x, mod, norm_g, w_in, conv_w, conv_b, w_gate, b_a, b_x, lam, sc_conv_w, w_out):
    seq, d = x.shape
    ts = ROW_TILE
    n_in = w_in.shape[1]
    n_steps = seq // (ts * TILES_PER_STEP)
    row2 = lambda v: v.reshape(1, -1)
    x_cur, x_next = _tile_specs(ts, d, n_steps)
    return pl.pallas_call(
        _even_kernel,
        grid=(n_steps,),
        in_specs=[
            x_cur, x_next,
            _const_spec((1, 3 * d)),
            _const_spec((1, d)),
            _const_spec(w_in.shape),
            _const_spec(conv_w.shape),
            _const_spec((1, d)),
            _const_spec(w_gate.shape),
            _const_spec((1, d)),
            _const_spec((1, d)),
            _const_spec((1, d)),
            _const_spec(sc_conv_w.shape),
            _const_spec(w_out.shape),
        ],
        out_specs=pl.BlockSpec((TILES_PER_STEP * ts, d), lambda i: (i, 0)),
        out_shape=jax.ShapeDtypeStruct((seq, d), F32),
        scratch_shapes=[
            pltpu.VMEM((ts, d), BF16),
            pltpu.VMEM((ts, d), BF16),
            pltpu.VMEM((ts, n_in), F32),
            pltpu.VMEM((ts, n_in), F32),
            pltpu.VMEM((ts, d), F32),
            pltpu.VMEM((ts, 2 * d), F32),
            pltpu.VMEM((ts, 2 * d), BF16),
            pltpu.VMEM((SUBLANES, d), F32),
            pltpu.VMEM((SUBLANES, d), F32),
            pltpu.VMEM((SUBLANES, d), F32),
        ],
        compiler_params=pltpu.CompilerParams(
            dimension_semantics=("arbitrary",), vmem_limit_bytes=VMEM_LIMIT_BYTES),
        name="even_layer",
    )(x, x, mod, row2(norm_g), w_in, conv_w, row2(conv_b), w_gate, row2(b_a), row2(b_x), row2(lam),
      sc_conv_w, w_out)


def _odd_kernel(x_ref, xn_ref, mod_ref, g_ref, win_ref, wgrp_ref, bgrp_ref, pscale_ref, wout_ref,
                fg_ref, o_ref,
                h_a, h_b, proj_a, proj_b, diff_buf, mixed_buf, y_buf, tails):
    d = x_ref.shape[1]
    ts = proj_a.shape[0]
    n_pool, gd, _ = wgrp_ref.shape
    width = n_pool * gd
    n_units = ts // PACKED_ROWS
    first_step = pl.program_id(0) == 0

    shift = mod_ref[:, 0:d]
    scale = mod_ref[:, d:2 * d]
    gate = mod_ref[:, 2 * d:3 * d]
    gmul_n = _bcast_rows(g_ref[...] * (1.0 + scale), PACKED_ROWS)
    shift_n = _bcast_rows(shift, PACKED_ROWS)

    row = lax.broadcasted_iota(jnp.int32, (SUBLANES, gd), 0)

    def pool_units(proj, at_seq_start):
        per_group = []
        tail_at = 0
        for k, window in enumerate(POOL_WINDOWS):
            levels = window.bit_length() - 1
            cols = slice(k * gd, (k + 1) * gd)
            tail_rows = [slice((tail_at + l) * SUBLANES, (tail_at + l + 1) * SUBLANES)
                         for l in range(levels)]
            tail_at += levels
            prev = [None] * levels

            def pool_unit(m, prev=prev, cols=cols, window=window, levels=levels, tail_rows=tail_rows):
                def run():
                    if m == 0:
                        for l in range(levels):
                            t = tails[tail_rows[l], :]
                            prev[l] = pltpu.roll(t, 1 << l, 0) if (1 << l) < SUBLANES else t
                    out = []
                    for g in (2 * m, 2 * m + 1):
                        rows = slice(g * SUBLANES, (g + 1) * SUBLANES)
                        v = proj[rows, cols]
                        cur = v
                        for l in range(levels):
                            s = 1 << l
                            if g == ts // SUBLANES - 1:
                                tails[tail_rows[l], :] = cur
                            if s < SUBLANES:
                                rolled = pltpu.roll(cur, s, 0)
                                nxt = cur + jnp.where(row >= s, rolled, prev[l])
                                prev[l] = rolled
                            else:
                                nxt = cur + prev[l]
                                prev[l] = cur
                            cur = nxt
                        if at_seq_start and g * SUBLANES < window - 1:
                            count = jnp.minimum(row + (g * SUBLANES + 1), window).astype(F32)
                            out.append(cur / count - v)
                        else:
                            out.append(cur * (1.0 / window) - v)
                    diff_buf[m * PACKED_ROWS:(m + 1) * PACKED_ROWS, cols] = (
                        jnp.concatenate(out, axis=0).astype(BF16))
                return (10 * levels + 12, run)

            per_group.append([pool_unit(m) for m in range(n_units)])
        return [u for per_m in zip(*per_group) for u in per_m]

    @pl.when(first_step)
    def _():
        tails[...] = jnp.zeros_like(tails)
        _run(_norm_chunks(x_ref, 0, h_b, gmul_n, shift_n, ts))
        proj_a[...] = jnp.dot(h_b[...], win_ref[...], preferred_element_type=F32)
        _run(_norm_chunks(x_ref, ts, h_a, gmul_n, shift_n, ts))
        _run(pool_units(proj_a, True))

    bgrp = _bcast_rows(bgrp_ref[...], PACKED_ROWS)
    pscale = _bcast_rows(pscale_ref[...], PACKED_ROWS)
    fg = fg_ref[...]

    def tile(x_row0, proj, proj_next, h_cur, h_next, next_row0, deferred):
        in_pieces = _dot_pieces(proj_next, h_cur, win_ref, 560)

        def grp_piece(k, p):
            cols = slice(k * gd + p * N_PIECE, k * gd + (p + 1) * N_PIECE)
            def run():
                mixed_buf[:, cols] = jnp.dot(diff_buf[:, k * gd:(k + 1) * gd],
                                             wgrp_ref[k, :, p * N_PIECE:(p + 1) * N_PIECE],
                                             preferred_element_type=F32)
            return (140, run)

        grp_pieces = [grp_piece(k, p) for k in range(n_pool) for p in range(gd // N_PIECE)]

        def gating_unit(m):
            rows = slice(m * PACKED_ROWS, (m + 1) * PACKED_ROWS)
            def run():
                y = (mixed_buf[rows, :] + bgrp) * pscale * _silu(proj[rows, width:2 * width])
                y_buf[rows, :] = y.astype(BF16)
            return (80, run)

        def out_piece(p):
            cols = slice(p * N_PIECE, (p + 1) * N_PIECE)
            rows = slice(x_row0, x_row0 + ts)
            def run():
                y = jnp.dot(y_buf[...], wout_ref[:, cols], preferred_element_type=F32)
                o_ref[rows, cols] = x_ref[rows, cols] + gate[:, cols] * y
            return (1100, run)

        def final_chunk(r):
            rows = slice(x_row0 + r * PACKED_ROWS, x_row0 + (r + 1) * PACKED_ROWS)
            def run():
                x2 = o_ref[rows, :]
                ms = jnp.mean(x2 * x2, axis=-1, keepdims=True)
                o_ref[rows, :] = x2 * lax.rsqrt(ms + EPS) * fg
            return (25, run)

        _interleave(grp_pieces, _norm_chunks(xn_ref, next_row0, h_next, gmul_n, shift_n, ts))
        n_v = width // N_PIECE
        pool = pool_units(proj_next, False)
        early = (3 * len(pool)) // 5
        _interleave(in_pieces[:n_v], [gating_unit(m) for m in range(n_units)] + deferred)
        _interleave(in_pieces[n_v:], pool[:early])
        _interleave([out_piece(p) for p in range(d // N_PIECE)], pool[early:])
        return [final_chunk(r) for r in range(ts // PACKED_ROWS)]

    final_a = tile(0, proj_a, proj_b, h_a, h_b, 0, [])
    _run(tile(ts, proj_b, proj_a, h_b, h_a, ts, final_a))


def _odd_layer(x, mod, norm_g, w_in, w_grp, b_grp, p_scale, w_out, final_g):
    seq, d = x.shape
    ts = ROW_TILE
    n_pool, gd, _ = w_grp.shape
    width = n_pool * gd
    n_steps = seq // (ts * TILES_PER_STEP)
    row2 = lambda v: v.reshape(1, -1)
    n_tails = sum(w.bit_length() - 1 for w in POOL_WINDOWS)
    x_cur, x_next = _tile_specs(ts, d, n_steps)
    return pl.pallas_call(
        _odd_kernel,
        grid=(n_steps,),
        in_specs=[
            x_cur, x_next,
            _const_spec((1, 3 * d)),
            _const_spec((1, d)),
            _const_spec(w_in.shape),
            _const_spec(w_grp.shape),
            _const_spec((1, width)),
            _const_spec((1, width)),
            _const_spec(w_out.shape),
            _const_spec((1, d)),
        ],
        out_specs=pl.BlockSpec((TILES_PER_STEP * ts, d), lambda i: (i, 0)),
        out_shape=jax.ShapeDtypeStruct((seq, d), F32),
        scratch_shapes=[
            pltpu.VMEM((ts, d), BF16),
            pltpu.VMEM((ts, d), BF16),
            pltpu.VMEM((ts, 2 * width), F32),
            pltpu.VMEM((ts, 2 * width), F32),
            pltpu.VMEM((ts, width), BF16),
            pltpu.VMEM((ts, width), F32),
            pltpu.VMEM((ts, width), BF16),
            pltpu.VMEM((n_tails * SUBLANES, gd), F32),
        ],
        compiler_params=pltpu.CompilerParams(
            dimension_semantics=("arbitrary",), vmem_limit_bytes=VMEM_LIMIT_BYTES),
        name="odd_layer",
    )(x, x, mod, row2(norm_g), w_in, w_grp, row2(b_grp), row2(p_scale), w_out, row2(final_g))


def kernel(x, c, norm_g, mod_w, mod_b, hy_w_in, hy_conv_w, hy_conv_b, lru_w_a, lru_b_a, lru_w_x, lru_b_x, lru_lambda, sc_conv_w, hy_w_out, pool_w_in, pool_w_grp, pool_b_grp, pool_scale, pool_w_out, final_g):
    batch, seq, d = x.shape
    assert batch == 1 and mod_w.shape[0] == 2 and seq % (ROW_TILE * TILES_PER_STEP) == 0
    mods = _modulation(c, mod_w, mod_b)
    w_gate = jnp.concatenate([lru_w_a[0], lru_w_x[0]], axis=-1).astype(BF16)
    x1 = _even_layer(x[0], mods[0], norm_g[0], hy_w_in[0].astype(BF16), hy_conv_w[0], hy_conv_b[0],
                     w_gate, lru_b_a[0], lru_b_x[0], lru_lambda[0], sc_conv_w[0],
                     hy_w_out[0].astype(BF16))
    out = _odd_layer(x1, mods[1], norm_g[1], pool_w_in[0].astype(BF16), pool_w_grp[0].astype(BF16),
                     pool_b_grp[0], pool_scale[0], pool_w_out[0].astype(BF16), final_g)
    return out[None]
```
